```python
import math
import jax, jax.numpy as jnp
from jax import lax
import numpy as np

D_MODEL = 1024
BATCH = 8
SEQ = 2048
DEPTH = 1

N_META = 16
D_MIX = D_MODEL
EPS = 1e-6
DIFF_HEADS = 4
DIFF_QK_DIM = 64
DIFF_V_DIM = 2 * DIFF_QK_DIM
DIFF_WIDTH = DIFF_HEADS * DIFF_V_DIM
ATTN_BLOCK = 128
REL_BUCKETS = 32
REL_MAX_DIST = 128
HGRN_HEADS = 4
HGRN_EXPAND = 128
HGRN_WIDTH = D_MIX - DIFF_WIDTH
HGRN_HEAD_V = HGRN_WIDTH // HGRN_HEADS
HGRN_KDIM = HGRN_HEADS * HGRN_EXPAND
HGRN_CHUNK = 16
COL_SIZES = [DIFF_HEADS * 2 * DIFF_QK_DIM,
             DIFF_HEADS * 2 * DIFF_QK_DIM,
             DIFF_WIDTH,
             HGRN_KDIM,
             HGRN_KDIM,
             HGRN_WIDTH,
             HGRN_WIDTH]
IN_COLS = sum(COL_SIZES)
SPLITS = [int(s) for s in np.cumsum(COL_SIZES)[:-1]]
D_FF = 2816
CONV_WIDTH = 3

kernel_name = "hymba_diffattn_hgrn2_convffn_layer"


def rmsnorm(x, g):
    xf = x.astype(jnp.float32)
    y = xf * lax.rsqrt(jnp.mean(xf * xf, axis=-1, keepdims=True) + EPS)
    return (y * g.astype(jnp.float32)).astype(x.dtype)


def rel_bucket(dist):
    n = jnp.maximum(dist, 0)
    max_exact = REL_BUCKETS // 2
    nf = jnp.maximum(n, 1).astype(jnp.float32)
    large = max_exact + (jnp.log(nf / max_exact) / math.log(REL_MAX_DIST / max_exact)
                         * (REL_BUCKETS - max_exact)).astype(jnp.int32)
    large = jnp.minimum(large, REL_BUCKETS - 1)
    return jnp.where(n < max_exact, n, large)


def diff_attention(q, k, v, lam_vecs, subln, rel_table, lam_init):
    B, L, _ = q.shape
    nblk = -(-L // ATTN_BLOCK)
    Lp = nblk * ATTN_BLOCK
    pad = Lp - L
    f32 = jnp.float32
    lv = lam_vecs.astype(f32)
    lam = jnp.exp(jnp.sum(lv[0] * lv[1])) - jnp.exp(jnp.sum(lv[2] * lv[3])) + lam_init
    qh = q.reshape(B, L, DIFF_HEADS, 2, DIFF_QK_DIM).transpose(0, 2, 3, 1, 4)
    kh = k.reshape(B, L, DIFF_HEADS, 2, DIFF_QK_DIM).transpose(0, 2, 3, 1, 4)
    vh = v.reshape(B, L, DIFF_HEADS, DIFF_V_DIM).transpose(0, 2, 1, 3)
    qh = jnp.pad(qh, ((0, 0), (0, 0), (0, 0), (0, pad), (0, 0)))
    kh = jnp.pad(kh, ((0, 0), (0, 0), (0, 0), (0, pad), (0, 0)))
    vh = jnp.pad(vh, ((0, 0), (0, 0), (0, pad), (0, 0))).astype(f32)
    qb = jnp.moveaxis(qh.reshape(B, DIFF_HEADS, 2, nblk, ATTN_BLOCK, DIFF_QK_DIM), 3, 0)
    k_pos = jnp.arange(Lp)
    scale = DIFF_QK_DIM ** -0.5

    def block(args):
        i, qblk = args
        q_pos = i * ATTN_BLOCK + jnp.arange(ATTN_BLOCK)
        dist = q_pos[:, None] - k_pos[None, :]
        bias = jnp.moveaxis(rel_table.astype(f32)[rel_bucket(dist)], -1, 0)
        s = jnp.einsum('bhmqd,bhmkd->bhmqk', qblk, kh).astype(f32) * scale + bias[None, :, None]
        s = jnp.where(dist >= 0, s, -jnp.inf)
        p = jax.nn.softmax(s, axis=-1)
        attn = p[:, :, 0] - lam * p[:, :, 1]
        return jnp.einsum('bhqk,bhkd->bhqd', attn, vh)

    o = lax.map(block, (jnp.arange(nblk), qb))
    o = jnp.moveaxis(o, 0, 2).reshape(B, DIFF_HEADS, Lp, DIFF_V_DIM)[:, :, :L]
    o = rmsnorm(o, subln) * (1.0 - lam_init)
    return o.transpose(0, 2, 1, 3).reshape(B, L, DIFF_WIDTH).astype(v.dtype)


def hgrn2(q, f_logit, i_in, g, lb, gnorm):
    B, L, _ = q.shape
    nC = L // HGRN_CHUNK
    f32 = jnp.float32
    f = lb + (1.0 - lb) * jax.nn.sigmoid(f_logit.astype(f32))
    log_f = jnp.log(f)
    kk = 1.0 - f

    def to_chunks(t, d):
        t = t.astype(f32).reshape(B, nC, HGRN_CHUNK, HGRN_HEADS, d)
        return jnp.transpose(t, (1, 0, 3, 2, 4))

    qc = to_chunks(q, HGRN_EXPAND) * (HGRN_EXPAND ** -0.5)
    kc = to_chunks(kk, HGRN_EXPAND)
    gc = to_chunks(log_f, HGRN_EXPAND)
    vc = to_chunks(i_in, HGRN_HEAD_V)
    causal = jnp.tril(jnp.ones((HGRN_CHUNK, HGRN_CHUNK), dtype=bool))

    def step(S, inp):
        qh, kh, vh, gh = inp
        G = jnp.cumsum(gh, axis=2)
        o_inter = jnp.einsum('bhtk,bhkv->bhtv', qh * jnp.exp(G), S)
        diff = G[:, :, :, None, :] - G[:, :, None, :, :]
        decay = jnp.exp(jnp.where(causal[:, :, None], diff, -jnp.inf))
        A = jnp.einsum('bhtk,bhsk,bhtsk->bhts', qh, kh, decay)
        o = o_inter + jnp.einsum('bhts,bhsv->bhtv', A, vh)
        G_last = G[:, :, -1]
        S_new = jnp.exp(G_last)[..., None] * S + jnp.einsum(
            'bhsk,bhsv->bhkv', kh * jnp.exp(G_last[:, :, None] - G), vh)
        return S_new, o

    S0 = jnp.zeros((B, HGRN_HEADS, HGRN_EXPAND, HGRN_HEAD_V), f32)
    _, o = lax.scan(step, S0, (qc, kc, vc, gc))
    o = jnp.transpose(o, (1, 0, 3, 2, 4)).reshape(B, L, HGRN_HEADS, HGRN_HEAD_V)
    gate = jax.nn.silu(g.astype(f32).reshape(B, L, HGRN_HEADS, HGRN_HEAD_V))
    o = rmsnorm(o, gnorm) * gate
    return o.reshape(B, L, HGRN_WIDTH).astype(q.dtype)


def conv_ffn(a, w_up, conv_w, conv_b, w_down):
    u = a @ w_up
    C = u.shape[-1]
    u = lax.conv_general_dilated(
        u, conv_w[:, None, :].astype(u.dtype), window_strides=(1,),
        padding=[(CONV_WIDTH - 1, 0)], dimension_numbers=('NWC', 'WIO', 'NWC'),
        feature_group_count=C) + conv_b
    gate, up = jnp.split(u, 2, axis=-1)
    return (jax.nn.gelu(gate, approximate=True) * up) @ w_down


def setup_inputs(seed: int = 0) -> dict:
    key = jax.random.key(seed)
    ks = jax.random.split(key, 17)
    f32 = jnp.float32

    def nrm(k, shape, s):
        return jax.random.normal(k, shape, f32) * s

    def gain(k, shape):
        return 1.0 + 0.02 * jax.random.normal(k, shape, f32)

    return {
        "x": nrm(ks[0], (BATCH, SEQ, D_MODEL), 1.0),
        "meta_tokens": nrm(ks[1], (N_META, D_MODEL), 1.0),
        "rel_bias_table": nrm(ks[2], (REL_BUCKETS, DIFF_HEADS), 0.5),
        "hgrn_lb_logits": nrm(ks[3], (DEPTH + 1, HGRN_KDIM), 0.1),
        "ln_mix_pre": gain(ks[4], (DEPTH, D_MODEL)),
        "ln_mix_post": gain(ks[5], (DEPTH, D_MODEL)),
        "w_in": nrm(ks[6], (DEPTH, D_MODEL, IN_COLS), D_MODEL ** -0.5),
        "diff_lambda": nrm(ks[7], (DEPTH, 4, DIFF_QK_DIM), 0.1),
        "diff_subln": gain(ks[8], (DEPTH, DIFF_V_DIM)),
        "hgrn_gnorm": gain(ks[9], (DEPTH, HGRN_HEAD_V)),
        "w_out": nrm(ks[10], (DEPTH, D_MIX, D_MODEL), D_MIX ** -0.5),
        "ln_ffn_pre": gain(ks[11], (DEPTH, D_MODEL)),
        "ln_ffn_post": gain(ks[12], (DEPTH, D_MODEL)),
        "w_ffn_up": nrm(ks[13], (DEPTH, D_MODEL, 2 * D_FF), D_MODEL ** -0.5),
        "ffn_conv_w": nrm(ks[14], (DEPTH, CONV_WIDTH, 2 * D_FF), CONV_WIDTH ** -0.5),
        "ffn_conv_b": nrm(ks[15], (DEPTH, 2 * D_FF), 0.01),
        "w_ffn_down": nrm(ks[16], (DEPTH, D_FF, D_MODEL), D_FF ** -0.5),
    }


def reference(x, meta_tokens, rel_bias_table, hgrn_lb_logits, ln_mix_pre, ln_mix_post,
              w_in, diff_lambda, diff_subln, hgrn_gnorm, w_out, ln_ffn_pre, ln_ffn_post,
              w_ffn_up, ffn_conv_w, ffn_conv_b, w_ffn_down):
    B = x.shape[0]
    meta = jnp.broadcast_to(meta_tokens[None].astype(x.dtype), (B, N_META, D_MODEL))
    h = jnp.concatenate([meta, x], axis=1)
    lb_all = jnp.cumsum(jax.nn.softmax(hgrn_lb_logits.astype(jnp.float32), axis=0), axis=0)
    for l in range(DEPTH):
        a = rmsnorm(h, ln_mix_pre[l])
        proj = a @ w_in[l]
        dq, dk, dv, hq, hf, hi, hg = jnp.split(proj, SPLITS, axis=-1)
        lam_init = 0.8 - 0.6 * math.exp(-0.3 * l)
        y_diff = diff_attention(dq, dk, dv, diff_lambda[l], diff_subln[l], rel_bias_table, lam_init)
        y_hgrn = hgrn2(hq, hf, hi, hg, lb_all[l], hgrn_gnorm[l])
        mix = jnp.concatenate([y_diff, y_hgrn], axis=-1) @ w_out[l]
        h = h + rmsnorm(mix, ln_mix_post[l])
        a = rmsnorm(h, ln_ffn_pre[l])
        y = conv_ffn(a, w_ffn_up[l], ffn_conv_w[l], ffn_conv_b[l], w_ffn_down[l])
        h = h + rmsnorm(y, ln_ffn_post[l])
    return h[:, N_META:, :]
```

```python
import functools
import math

import jax
import jax.numpy as jnp
from jax import lax
from jax.experimental import pallas as pl
from jax.experimental.pallas import tpu as pltpu

F32 = jnp.float32
BF16 = jnp.bfloat16

D_MODEL = 1024
N_META = 16
EPS = 1e-6
HEADS = 4
HEAD_DIM = 128
QK_DIM = 64
GROUP_W = HEADS * HEAD_DIM
N_GROUPS = 7
D_FF = 2816
CONV_WIDTH = 3
REL_BUCKETS = 32
REL_MAX_DIST = 128
LAM_INIT = 0.8 - 0.6 * math.exp(-0.3 * 0)

LOG2E = 1.4426950408889634
NEG = -1e30

LANES = 128
SUBLANES = 8

SEQ_PAD = 2304
ROW_TILE = 512
ATTN_TILE = 256
HGRN_CHUNK = 128
FF_CHUNK = 256
VMEM_LIMIT = 56 * 1024 * 1024


def _nt_dot(a, b):
    return lax.dot_general(a, b, (((1,), (1,)), ((), ())), preferred_element_type=F32)


def _tn_dot(a, b):
    return lax.dot_general(a, b, (((0,), (0,)), ((), ())), preferred_element_type=F32)


def _dot(a, b):
    return jnp.dot(a, b, preferred_element_type=F32)


def _rms(x):
    return x * lax.rsqrt(jnp.mean(x * x, axis=-1, keepdims=True) + EPS)


def _inproj_kernel(x_ref, g_ref, lbl_ref, w_ref,
                   dq_ref, dk_ref, dv_ref, hq_ref, lf_ref, hi_ref, gate_ref):
    a = (_rms(x_ref[...]) * g_ref[...]).astype(BF16)

    def proj(i):
        return _dot(a, w_ref[:, i * GROUP_W:(i + 1) * GROUP_W])

    dq_ref[...] = (proj(0) * (QK_DIM ** -0.5 * LOG2E)).astype(BF16)
    dk_ref[...] = proj(1).astype(BF16)
    dv_ref[...] = proj(2).astype(BF16)
    hq_ref[...] = (proj(3) * (HEAD_DIM ** -0.5)).astype(BF16)
    l0 = lbl_ref[0:1, :]
    l1 = lbl_ref[1:2, :]
    mx = jnp.maximum(l0, l1)
    e0 = jnp.exp(l0 - mx)
    e1 = jnp.exp(l1 - mx)
    lb = e0 / (e0 + e1)
    f = lb + (1.0 - lb) * (1.0 / (1.0 + jnp.exp(-proj(4))))
    lf_ref[...] = jnp.log(f)
    hi_ref[...] = proj(5).astype(BF16)
    gz = proj(6)
    gate_ref[...] = (gz * (1.0 / (1.0 + jnp.exp(-gz)))).astype(BF16)


def _inproj(h2d, g_pre, lb_logits, w_in):
    rows = h2d.shape[0]
    grid = (rows // ROW_TILE,)
    row_spec = lambda w: pl.BlockSpec((ROW_TILE, w), lambda i: (i, 0))
    full = lambda s: pl.BlockSpec(s, lambda i: (0, 0))
    out_dtypes = (BF16, BF16, BF16, BF16, F32, BF16, BF16)
    return pl.pallas_call(
        _inproj_kernel,
        out_shape=tuple(jax.ShapeDtypeStruct((rows, GROUP_W), dt) for dt in out_dtypes),
        grid=grid,
        in_specs=[row_spec(D_MODEL), full((1, D_MODEL)), full((2, GROUP_W)),
                  full((D_MODEL, N_GROUPS * GROUP_W))],
        out_specs=tuple(row_spec(GROUP_W) for _ in out_dtypes),
        compiler_params=pltpu.CompilerParams(
            dimension_semantics=("arbitrary",), vmem_limit_bytes=VMEM_LIMIT),
        name="inproj",
    )(h2d, g_pre, lb_logits, w_in)


def _relbias_kernel(tab_ref, o_ref, *, tile):
    h = pl.program_id(0)
    r = lax.broadcasted_iota(jnp.int32, (tile, tile), 0)
    c = lax.broadcasted_iota(jnp.int32, (tile, tile), 1)
    far = tab_ref[REL_BUCKETS - 1, h]
    max_exact = REL_BUCKETS // 2
    for idx, d0 in enumerate((0, tile)):
        dist = d0 + r - c
        n = jnp.maximum(dist, 0)
        nf = jnp.maximum(n, 1).astype(F32)
        large = max_exact + (jnp.log(nf / max_exact) / math.log(REL_MAX_DIST / max_exact)
                             * (REL_BUCKETS - max_exact)).astype(jnp.int32)
        large = jnp.minimum(large, REL_BUCKETS - 1)
        bucket = jnp.where(n < max_exact, n, large)
        val = jnp.zeros((tile, tile), F32)
        for b in range(REL_BUCKETS):
            val = jnp.where(bucket == b, tab_ref[b, h], val)
        val = (val - far) * LOG2E
        if d0 == 0:
            val = jnp.where(dist >= 0, val, NEG)
        o_ref[0, idx] = val
    o_ref[0, 2] = jnp.zeros((tile, tile), F32)


def _relbias(rel_table, tile):
    return pl.pallas_call(
        functools.partial(_relbias_kernel, tile=tile),
        out_shape=jax.ShapeDtypeStruct((HEADS, 3, tile, tile), F32),
        grid=(HEADS,),
        in_specs=[pl.BlockSpec(memory_space=pltpu.SMEM)],
        out_specs=pl.BlockSpec((1, 3, tile, tile), lambda h: (h, 0, 0, 0)),
        compiler_params=pltpu.CompilerParams(dimension_semantics=("arbitrary",)),
        name="relbias",
    )(rel_table)


def _attn_kernel(lam_ref, sub_ref, q_ref, k_ref, v_ref, bias_ref, o_ref, m_scr, acc_scr, *, tile):
    i = pl.program_id(2)
    q = q_ref[0]
    lane = lax.broadcasted_iota(jnp.int32, (tile, HEAD_DIM), 1)
    zero = jnp.zeros_like(q)
    q_maps = (jnp.where(lane < QK_DIM, q, zero), jnp.where(lane >= QK_DIM, q, zero))
    m_scr[...] = jnp.full(m_scr.shape, NEG, F32)
    acc_scr[...] = jnp.zeros(acc_scr.shape, F32)
    ones = jnp.ones((tile, HEAD_DIM), BF16)

    def body(j, carry):
        k0 = pl.multiple_of(j * tile, tile)
        kt = k_ref[0, pl.ds(k0, tile), :]
        vext = jnp.concatenate([v_ref[0, pl.ds(k0, tile), :], ones], axis=1)
        bias = bias_ref[0, jnp.minimum(i - j, 2)]
        for mi in range(2):
            s = _nt_dot(q_maps[mi], kt) + bias
            m_prev = m_scr[mi]
            m_next = jnp.maximum(m_prev, jnp.max(s, axis=1, keepdims=True))
            p = jnp.exp2(s - jnp.tile(m_next, (1, tile // LANES)))
            alpha = jnp.exp2(m_prev - m_next)
            acc_scr[mi] = acc_scr[mi] * jnp.tile(alpha, (1, 2)) + _dot(p.astype(BF16), vext)
            m_scr[mi] = m_next
        return carry

    lax.fori_loop(0, i + 1, body, 0)

    lv = lam_ref[...]
    lam = (jnp.exp(jnp.sum(lv[0:1] * lv[1:2], axis=1, keepdims=True))
           - jnp.exp(jnp.sum(lv[2:3] * lv[3:4], axis=1, keepdims=True)) + LAM_INIT)
    a1 = acc_scr[0]
    a2 = acc_scr[1]
    o = a1[:, :HEAD_DIM] / a1[:, HEAD_DIM:] - lam * (a2[:, :HEAD_DIM] / a2[:, HEAD_DIM:])
    o_ref[0] = (_rms(o) * sub_ref[...] * (1.0 - LAM_INIT)).astype(BF16)


def _attention(dq, dk, dv, bias, lam_vecs, subln, tile):
    bsz, seq, _ = dq.shape
    nq = seq // tile
    return pl.pallas_call(
        functools.partial(_attn_kernel, tile=tile),
        out_shape=jax.ShapeDtypeStruct((bsz, seq, GROUP_W), BF16),
        grid=(HEADS, bsz, nq),
        in_specs=[
            pl.BlockSpec((4, QK_DIM), lambda h, b, i: (0, 0)),
            pl.BlockSpec((1, HEAD_DIM), lambda h, b, i: (0, 0)),
            pl.BlockSpec((1, tile, HEAD_DIM), lambda h, b, i: (b, i, h)),
            pl.BlockSpec((1, seq, HEAD_DIM), lambda h, b, i: (b, 0, h)),
            pl.BlockSpec((1, seq, HEAD_DIM), lambda h, b, i: (b, 0, h)),
            pl.BlockSpec((1, 3, tile, tile), lambda h, b, i: (h, 0, 0, 0)),
        ],
        out_specs=pl.BlockSpec((1, tile, HEAD_DIM), lambda h, b, i: (b, i, h)),
        scratch_shapes=[pltpu.VMEM((2, tile, LANES), F32),
                        pltpu.VMEM((2, tile, 2 * HEAD_DIM), F32)],
        compiler_params=pltpu.CompilerParams(
            dimension_semantics=("arbitrary", "arbitrary", "arbitrary"),
            vmem_limit_bytes=VMEM_LIMIT),
        name="diffattn",
    )(lam_vecs, subln, dq, dk, dv, bias)


def _ref_rows(g_cum, half):
    chunk, width = g_cum.shape
    if 2 * half >= SUBLANES:
        pieces = []
        for blk in range(chunk // (2 * half)):
            r = blk * 2 * half + half - 1
            pieces.append(jnp.broadcast_to(g_cum[r:r + 1, :], (2 * half, width)))
        return jnp.concatenate(pieces, axis=0)
    sub = lax.broadcasted_iota(jnp.int32, (SUBLANES, width), 0)
    pieces = []
    for grp in range(chunk // SUBLANES):
        acc = None
        for blk in range(SUBLANES // (2 * half)):
            r = grp * SUBLANES + blk * 2 * half + half - 1
            bc = jnp.broadcast_to(g_cum[r:r + 1, :], (SUBLANES, width))
            acc = bc if acc is None else jnp.where(sub >= blk * 2 * half, bc, acc)
        pieces.append(acc)
    return jnp.concatenate(pieces, axis=0)


def _hgrn_kernel(q_ref, lf_ref, v_ref, gate_ref, s0_ref, gn_ref, y_ref, sT_ref, lvl_scr,
                 *, chunk, n_chunks):
    n_levels = chunk.bit_length() - 1
    row = lax.broadcasted_iota(jnp.int32, (chunk, chunk), 0)
    col = lax.broadcasted_iota(jnp.int32, (chunk, chunk), 1)
    x = row ^ col
    lvl = jnp.full((chunk, chunk), -1, jnp.int32)
    for b in range(n_levels):
        lvl = lvl + (jnp.right_shift(x, b) != 0).astype(jnp.int32)
    lvl_scr[...] = jnp.where(row > col, lvl, jnp.where(row == col, n_levels, -1))
    sT_ref[0, 0] = s0_ref[0]

    def body(c, carry):
        r0 = pl.multiple_of(c * chunk, chunk)
        rows = pl.ds(r0, chunk)
        g = lf_ref[0, rows, :] * LOG2E
        q = q_ref[0, rows, :].astype(F32)
        v = v_ref[0, rows, :]
        kk = 1.0 - jnp.exp2(g)
        tr = lax.broadcasted_iota(jnp.int32, (chunk, chunk), 0)
        tc = lax.broadcasted_iota(jnp.int32, (chunk, chunk), 1)
        tril = jnp.where(tr >= tc, 1.0, 0.0).astype(BF16)
        g_hi = g.astype(BF16)
        g_lo = (g - g_hi.astype(F32)).astype(BF16)
        g_cum = _dot(tril, g_hi) + _dot(tril, g_lo)
        g_last = g_cum[chunk - 1:chunk, :]
        sT = sT_ref[0, 0]
        o = _nt_dot((q * jnp.exp2(g_cum)).astype(BF16), sT.astype(BF16))
        k_dec = (kk * jnp.exp2(g_last - g_cum)).astype(BF16)
        sT_ref[0, 0] = sT * jnp.exp2(g_last) + _tn_dot(v, k_dec)
        rsel = lax.broadcasted_iota(jnp.int32, (chunk, HEAD_DIM), 0)
        code = lvl_scr[...]
        att = jnp.where(code == n_levels, _nt_dot(q.astype(BF16), kk.astype(BF16)), 0.0)
        for level in range(n_levels):
            half = 1 << level
            dist = g_cum - _ref_rows(g_cum, half)
            w = jnp.where((rsel & half) != 0, q, kk) * jnp.exp2(jnp.minimum(dist, -dist))
            wb = w.astype(BF16)
            att = jnp.where(code == level, _nt_dot(wb, wb), att)
        o = o + _dot(att.astype(BF16), v)
        y = _rms(o) * gn_ref[...] * gate_ref[0, rows, :].astype(F32)
        y_ref[0, rows, :] = y.astype(BF16)
        return carry

    lax.fori_loop(0, n_chunks, body, 0)


def _hgrn(hq, lf, hi, gate, s0, gnorm, chunk):
    bsz, seq, _ = hq.shape
    seq_spec = pl.BlockSpec((1, seq, HEAD_DIM), lambda b, h: (b, 0, h))
    return pl.pallas_call(
        functools.partial(_hgrn_kernel, chunk=chunk, n_chunks=seq // chunk),
        out_shape=(jax.ShapeDtypeStruct((bsz, seq, GROUP_W), BF16),
                   jax.ShapeDtypeStruct((bsz, HEADS, HEAD_DIM, HEAD_DIM), F32)),
        grid=(bsz, HEADS),
        in_specs=[seq_spec, seq_spec, seq_spec, seq_spec,
                  pl.BlockSpec((1, HEAD_DIM, HEAD_DIM), lambda b, h: (h, 0, 0)),
                  pl.BlockSpec((1, HEAD_DIM), lambda b, h: (0, 0))],
        out_specs=(seq_spec,
                   pl.BlockSpec((1, 1, HEAD_DIM, HEAD_DIM), lambda b, h: (b, h, 0, 0))),
        scratch_shapes=[pltpu.VMEM((chunk, chunk), jnp.int32)],
        compiler_params=pltpu.CompilerParams(
            dimension_semantics=("arbitrary", "arbitrary"), vmem_limit_bytes=VMEM_LIMIT),
        name="hgrn2",
    )(hq, lf, hi, gate, s0, gnorm)


def _outproj_kernel(yd_ref, yh_ref, h_ref, w_ref, gpost_ref, gpre_ref, h1_ref, a2_ref):
    mix = _dot(yd_ref[...], w_ref[0:GROUP_W, :]) + _dot(yh_ref[...], w_ref[GROUP_W:2 * GROUP_W, :])
    h1 = h_ref[...] + _rms(mix) * gpost_ref[...]
    h1_ref[...] = h1
    a2_ref[...] = (_rms(h1) * gpre_ref[...]).astype(BF16)


def _outproj(yd, yh, h2d, w_out, g_post, g_pre):
    rows = h2d.shape[0]
    row_spec = lambda w: pl.BlockSpec((ROW_TILE, w), lambda i: (i, 0))
    full = lambda s: pl.BlockSpec(s, lambda i: (0, 0))
    return pl.pallas_call(
        _outproj_kernel,
        out_shape=(jax.ShapeDtypeStruct((rows, D_MODEL), F32),
                   jax.ShapeDtypeStruct((rows, D_MODEL), BF16)),
        grid=(rows // ROW_TILE,),
        in_specs=[row_spec(GROUP_W), row_spec(GROUP_W), row_spec(D_MODEL),
                  full((D_MODEL, D_MODEL)), full((1, D_MODEL)), full((1, D_MODEL))],
        out_specs=(row_spec(D_MODEL), row_spec(D_MODEL)),
        compiler_params=pltpu.CompilerParams(
            dimension_semantics=("arbitrary",), vmem_limit_bytes=VMEM_LIMIT),
        name="outproj",
    )(yd, yh, h2d, w_out, g_post, g_pre)


def _gelu_tanh(x):
    return 0.5 * x * (1.0 + jnp.tanh(math.sqrt(2.0 / math.pi) * (x + 0.044715 * (x * x * x))))


def _ffn_kernel(a_ref, h1_ref, wup_ref, cw_ref, cb_ref, wdn_ref, gpost_ref, carry0_ref,
                o_ref, ubuf, carry, yacc, *, tile, reset_period):
    t = pl.program_id(0)

    @pl.when(t % reset_period == 0)
    def _():
        carry[...] = carry0_ref[...]

    a = a_ref[...]
    yacc[...] = jnp.zeros(yacc.shape, F32)
    for c in range(D_FF // FF_CHUNK):
        halves = []
        for part in range(2):
            col0 = part * D_FF + c * FF_CHUNK
            cols = slice(col0, col0 + FF_CHUNK)
            u = _dot(a, wup_ref[:, cols])
            ubuf[part, 0:SUBLANES, :] = carry[:, cols]
            ubuf[part, SUBLANES:SUBLANES + tile, :] = u
            carry[:, cols] = u[tile - SUBLANES:tile, :]
            w = cw_ref[:, cols]
            conv = cb_ref[:, cols] + ubuf[part, SUBLANES:SUBLANES + tile, :] * w[2:3]
            for tap in range(CONV_WIDTH - 1):
                off = SUBLANES - (CONV_WIDTH - 1) + tap
                conv = conv + ubuf[part, off:off + tile, :] * w[tap:tap + 1]
            halves.append(conv)
        act = (_gelu_tanh(halves[0]) * halves[1]).astype(BF16)
        yacc[...] += _dot(act, wdn_ref[c * FF_CHUNK:(c + 1) * FF_CHUNK, :])
    o_ref[...] = h1_ref[...] + _rms(yacc[...]) * gpost_ref[...]


def _ffn(a2, h1, w_up, conv_w, conv_b, w_down, g_post, carry0, reset_period):
    rows = a2.shape[0]
    row_spec = lambda w: pl.BlockSpec((ROW_TILE, w), lambda i: (i, 0))
    full = lambda s: pl.BlockSpec(s, lambda i: (0, 0))
    return pl.pallas_call(
        functools.partial(_ffn_kernel, tile=ROW_TILE, reset_period=reset_period),
        out_shape=jax.ShapeDtypeStruct((rows, D_MODEL), F32),
        grid=(rows // ROW_TILE,),
        in_specs=[row_spec(D_MODEL), row_spec(D_MODEL), full((D_MODEL, 2 * D_FF)),
                  full((CONV_WIDTH, 2 * D_FF)), full((1, 2 * D_FF)), full((D_FF, D_MODEL)),
                  full((1, D_MODEL)), full((SUBLANES, 2 * D_FF))],
        out_specs=row_spec(D_MODEL),
        scratch_shapes=[pltpu.VMEM((2, ROW_TILE + SUBLANES, FF_CHUNK), F32),
                        pltpu.VMEM((SUBLANES, 2 * D_FF), F32),
                        pltpu.VMEM((ROW_TILE, D_MODEL), F32)],
        compiler_params=pltpu.CompilerParams(
            dimension_semantics=("arbitrary",), vmem_limit_bytes=VMEM_LIMIT),
        name="convffn",
    )(a2, h1, w_up, conv_w, conv_b, w_down, g_post, carry0)


def kernel(x, meta_tokens, rel_bias_table, hgrn_lb_logits, ln_mix_pre, ln_mix_post, w_in,
           diff_lambda, diff_subln, hgrn_gnorm, w_out, ln_ffn_pre, ln_ffn_post, w_ffn_up,
           ffn_conv_w, ffn_conv_b, w_ffn_down):
    bsz, seq, _ = x.shape
    meta = jnp.broadcast_to(meta_tokens[None].astype(x.dtype), (bsz, N_META, D_MODEL))
    pad = jnp.zeros((bsz, SEQ_PAD - N_META - seq, D_MODEL), x.dtype)
    h2d = jnp.concatenate([meta, x, pad], axis=1).reshape(bsz * SEQ_PAD, D_MODEL)
    rows = h2d.shape[0]

    dq, dk, dv, hq, lf, hi, gate = _inproj(
        h2d, ln_mix_pre[0][None], hgrn_lb_logits, w_in[0].astype(BF16))
    as3d = lambda t: t.reshape(bsz, SEQ_PAD, GROUP_W)

    bias = _relbias(rel_bias_table, ATTN_TILE)
    yd = _attention(as3d(dq), as3d(dk), as3d(dv), bias, diff_lambda[0], diff_subln[0][None],
                    ATTN_TILE)
    s0 = jnp.zeros((HEADS, HEAD_DIM, HEAD_DIM), F32)
    yh, _ = _hgrn(as3d(hq), as3d(lf), as3d(hi), as3d(gate), s0, hgrn_gnorm[0][None], HGRN_CHUNK)

    h1, a2 = _outproj(yd.reshape(rows, GROUP_W), yh.reshape(rows, GROUP_W), h2d,
                      w_out[0].astype(BF16), ln_mix_post[0][None], ln_ffn_pre[0][None])
    carry0 = jnp.zeros((SUBLANES, 2 * D_FF), F32)
    out = _ffn(a2, h1, w_ffn_up[0].astype(BF16), ffn_conv_w[0], ffn_conv_b[0][None],
               w_ffn_down[0].astype(BF16), ln_ffn_post[0][None], carry0,
               reset_period=rows // ROW_TILE)
    return out.reshape(bsz, SEQ_PAD, D_MODEL)[:, N_META:N_META + seq, :]
```
